```python
import math
import jax, jax.numpy as jnp
from jax import lax
import numpy as np

D_MODEL = 1024
BATCH = 2
SEQ = 8192
DEPTH = 1
DEC_BATCH = 32
DEC_SEQ = 1
PAST_LEN = 8192
PAGE_SIZE = 128

MIX_WIDTH = D_MODEL
ATT_WIDTH = MIX_WIDTH // 2
DH_A = 64
H_A = ATT_WIDTH // DH_A
ATT_PATTERNS = ((128, 1), (512, 4), (2048, 16))
WIN_MAX = max(w for w, _ in ATT_PATTERNS)
ATT_BLOCK = 128
GDN_WIDTH = MIX_WIDTH - ATT_WIDTH
DK = 128
DV = 128
H_B = GDN_WIDTH // DV
CONV_W = 4
C_CONV = H_B * (2 * DK + DV)
GDN_CHUNK = 64
PROJ_SIZES = (ATT_WIDTH, ATT_WIDTH, ATT_WIDTH, C_CONV, H_B * DV, H_B, H_B)
PROJ_WIDTH = sum(PROJ_SIZES)
N_EXPERTS = 256
TOP_K = 8
N_GROUP = 8
TOPK_GROUP = 4
D_EXPERT = D_MODEL // 4
D_SHARED = D_EXPERT
ROUTED_SCALE = 2.5
MOE_BLOCK = 128
ALPHA = (2 * DEPTH) ** 0.25
BETA_INIT = (8 * DEPTH) ** -0.25
LN_EPS = 1e-5
RMS_EPS = 1e-6

kernel_name = 'hymba_dilated_gdn_moe_deepnorm_step'


def _layer_norm(x, g, b):
    xf = x.astype(jnp.float32)
    mu = jnp.mean(xf, -1, keepdims=True)
    var = jnp.mean(jnp.square(xf - mu), -1, keepdims=True)
    return ((xf - mu) * lax.rsqrt(var + LN_EPS) * g.astype(jnp.float32) + b.astype(jnp.float32)).astype(x.dtype)


def _l2norm(x):
    xf = x.astype(jnp.float32)
    return xf * lax.rsqrt(jnp.sum(jnp.square(xf), -1, keepdims=True) + RMS_EPS)


def _alibi_slopes():
    return 2.0 ** (-8.0 * jnp.arange(1, H_A + 1, dtype=jnp.float32) / H_A)


def _branch_prompt(q, k, v, dil, span, slopes):
    b, s, h, c = q.shape
    sub_len = -(-s // (dil * ATT_BLOCK)) * ATT_BLOCK
    pad = sub_len * dil - s
    nblk = sub_len // ATT_BLOCK

    def to_blocks(t):
        t = jnp.pad(t, ((0, 0), (0, pad), (0, 0), (0, 0)))
        return t.reshape(b, nblk, ATT_BLOCK, dil, h, c)

    def with_prev(t):
        prev = jnp.pad(t[:, :-1], ((0, 0), (1, 0), (0, 0), (0, 0), (0, 0), (0, 0)))
        return jnp.concatenate([prev, t], axis=2)

    qb = to_blocks(q)
    kk = with_prev(to_blocks(k))
    vv = with_prev(to_blocks(v))
    scores = jnp.einsum('bnqrhc,bnkrhc->bnrhqk', qb, kk).astype(jnp.float32) * (c ** -0.5)
    qi = jnp.arange(ATT_BLOCK)[:, None]
    ki = jnp.arange(2 * ATT_BLOCK)[None, :]
    j = ATT_BLOCK + qi - ki
    band = (j >= 0) & (j <= span)
    real = (jnp.arange(nblk)[:, None, None] > 0) | (ki >= ATT_BLOCK)[None]
    mask = (band[None] & real)[None, :, None, None]
    bias = -slopes[:, None, None] * (j * dil).astype(jnp.float32)
    scores = jnp.where(mask, scores + bias, -jnp.inf)
    m = jnp.max(scores, -1, keepdims=True)
    p = jnp.exp(scores - m)
    den = jnp.sum(p, -1, keepdims=True)
    lse = (m + jnp.log(den))[..., 0]
    o = jnp.einsum('bnrhqk,bnkrhc->bnqrhc', (p / den).astype(v.dtype), vv).astype(jnp.float32)
    o = o.reshape(b, sub_len * dil, h, c)[:, :s]
    lse = jnp.transpose(lse, (0, 1, 4, 2, 3)).reshape(b, sub_len * dil, h)[:, :s]
    return o, lse


def _branch_sample(q, k_all, v_all, n_past, dil, span, slopes):
    n = q.shape[1]
    c = q.shape[-1]
    j = jnp.arange(span + 1)
    idx = n_past + jnp.arange(n)[:, None] - j[None, :] * dil
    valid = idx >= 0
    idx = jnp.maximum(idx, 0)
    kg = k_all[:, idx]
    vg = v_all[:, idx]
    scores = jnp.einsum('bqhc,bqjhc->bhqj', q, kg).astype(jnp.float32) * (c ** -0.5)
    bias = -slopes[:, None, None] * (j * dil).astype(jnp.float32)[None, None, :]
    scores = jnp.where(valid[None, None], scores + bias, -jnp.inf)
    m = jnp.max(scores, -1, keepdims=True)
    p = jnp.exp(scores - m)
    den = jnp.sum(p, -1, keepdims=True)
    lse = jnp.transpose((m + jnp.log(den))[..., 0], (0, 2, 1))
    o = jnp.einsum('bhqj,bqjhc->bqhc', (p / den).astype(vg.dtype), vg).astype(jnp.float32)
    return o, lse


def _dilated_attention(q, k, v, n_past=None):
    slopes = _alibi_slopes()
    outs, lses = [], []
    for window, dil in ATT_PATTERNS:
        span = window // dil
        if n_past is None:
            o, lse = _branch_prompt(q, k, v, dil, span, slopes)
        else:
            o, lse = _branch_sample(q, k, v, n_past, dil, span, slopes)
        outs.append(o)
        lses.append(lse)
    wts = jax.nn.softmax(jnp.stack(lses), axis=0)[..., None]
    return jnp.sum(wts * jnp.stack(outs), axis=0)


def _causal_conv(xpad, w):
    t = xpad.shape[1] - (CONV_W - 1)
    return sum(xpad[:, i:i + t] * w[i] for i in range(CONV_W))


def _gated_delta_chunked(q, k, v, g, beta, s0):
    b, t, h, _ = q.shape
    c = GDN_CHUNK
    n = t // c

    def chunks(x):
        return jnp.moveaxis(x.reshape((b, n, c) + x.shape[2:]), 3, 1)

    q, k, v, g, beta = chunks(q), chunks(k), chunks(v), chunks(g), chunks(beta)
    gc = jnp.cumsum(g, axis=-1)
    incl = jnp.tril(jnp.ones((c, c), bool))
    strict = jnp.tril(jnp.ones((c, c), bool), -1)
    decay = jnp.exp(jnp.where(incl, gc[..., :, None] - gc[..., None, :], -jnp.inf))
    kb = k * beta[..., None]
    a = jnp.where(strict, jnp.einsum('bhncd,bhnsd->bhncs', kb, k) * decay, 0.0)
    rhs = jnp.concatenate([v * beta[..., None], kb * jnp.exp(gc)[..., None]], axis=-1)
    sol = lax.linalg.triangular_solve(a + jnp.eye(c, dtype=a.dtype), rhs, left_side=True, lower=True,
                                      unit_diagonal=True)
    u, w = sol[..., :v.shape[-1]], sol[..., v.shape[-1]:]
    qk = jnp.einsum('bhncd,bhnsd->bhncs', q, k) * decay
    q_dec = q * jnp.exp(gc)[..., None]
    g_last = gc[..., -1]
    k_dec = k * jnp.exp(g_last[..., None] - gc)[..., None]

    def step(state, inp):
        u_n, w_n, qd_n, qk_n, kd_n, gl_n = inp
        v_new = u_n - jnp.einsum('bhcd,bhde->bhce', w_n, state)
        o = jnp.einsum('bhcd,bhde->bhce', qd_n, state) + jnp.einsum('bhcs,bhse->bhce', qk_n, v_new)
        state = state * jnp.exp(gl_n)[..., None, None] + jnp.einsum('bhcd,bhce->bhde', kd_n, v_new)
        return state, o

    xs = tuple(jnp.moveaxis(x, 2, 0) for x in (u, w, q_dec, qk, k_dec, g_last))
    s_fin, o = lax.scan(step, s0, xs)
    o = jnp.transpose(o, (1, 0, 3, 2, 4)).reshape(b, t, h, -1)
    return o, s_fin


def _gated_delta_recurrent(q, k, v, g, beta, s0):
    def step(state, inp):
        q_t, k_t, v_t, g_t, b_t = inp
        state = state * jnp.exp(g_t)[..., None, None]
        v_t = (v_t - jnp.einsum('bhd,bhde->bhe', k_t, state)) * b_t[..., None]
        state = state + jnp.einsum('bhd,bhe->bhde', k_t, v_t)
        return state, jnp.einsum('bhd,bhde->bhe', q_t, state)

    s_fin, o = lax.scan(step, s0, tuple(jnp.moveaxis(x, 1, 0) for x in (q, k, v, g, beta)))
    return jnp.moveaxis(o, 0, 1), s_fin


def _routed_experts(x, e_idx, gate, w_gate_e, w_up_e, w_down_e):
    n_tok, d = x.shape
    n_as = n_tok * TOP_K
    flat_e = e_idx.reshape(-1)
    order = jnp.argsort(flat_e)
    se = flat_e[order]
    stok = (order // TOP_K).astype(jnp.int32)
    sg = gate.reshape(-1)[order]
    counts = jax.ops.segment_sum(jnp.ones_like(flat_e), flat_e, num_segments=N_EXPERTS)
    padded = (counts + MOE_BLOCK - 1) // MOE_BLOCK * MOE_BLOCK
    pad_end = jnp.cumsum(padded)
    rank = jnp.arange(n_as) - (jnp.cumsum(counts) - counts)[se]
    dest = (pad_end - padded)[se] + rank
    n_blocks = -(-n_as // MOE_BLOCK) + N_EXPERTS
    rows = n_blocks * MOE_BLOCK
    row_tok = jnp.full((rows,), n_tok, jnp.int32).at[dest].set(stok)
    row_gate = jnp.zeros((rows,), jnp.float32).at[dest].set(sg)
    blk_e = jnp.minimum(jnp.searchsorted(pad_end, jnp.arange(n_blocks) * MOE_BLOCK, side='right'), N_EXPERTS - 1)
    x_pad = jnp.concatenate([x, jnp.zeros((1, d), x.dtype)], axis=0)

    def block(acc, inp):
        tok, g, e = inp
        xb = x_pad[tok]
        hdn = jax.nn.silu(xb @ w_gate_e[e]) * (xb @ w_up_e[e])
        return acc.at[tok].add((hdn @ w_down_e[e]).astype(jnp.float32) * g[:, None]), None

    acc, _ = lax.scan(block, jnp.zeros((n_tok + 1, d), jnp.float32),
                      (row_tok.reshape(n_blocks, MOE_BLOCK), row_gate.reshape(n_blocks, MOE_BLOCK), blk_e))
    return acc[:n_tok]


def _moe(x, w_router, router_bias, w_gate_e, w_up_e, w_down_e, w_gate_s, w_up_s, w_down_s):
    n_tok = x.shape[0]
    scores = jax.nn.sigmoid(x.astype(jnp.float32) @ w_router.astype(jnp.float32))
    choice = scores + router_bias.astype(jnp.float32)
    per_group = choice.reshape(n_tok, N_GROUP, N_EXPERTS // N_GROUP)
    group_score = jnp.sum(lax.top_k(per_group, 2)[0], -1)
    _, g_idx = lax.top_k(group_score, TOPK_GROUP)
    g_keep = jnp.sum(jax.nn.one_hot(g_idx, N_GROUP, dtype=jnp.float32), -2) > 0
    e_keep = jnp.repeat(g_keep, N_EXPERTS // N_GROUP, axis=-1)
    _, e_idx = lax.top_k(jnp.where(e_keep, choice, -jnp.inf), TOP_K)
    gate = jnp.take_along_axis(scores, e_idx, axis=-1)
    gate = gate / jnp.sum(gate, -1, keepdims=True) * ROUTED_SCALE
    routed = _routed_experts(x, e_idx, gate, w_gate_e, w_up_e, w_down_e)
    shared = (jax.nn.silu(x @ w_gate_s) * (x @ w_up_s)) @ w_down_s
    return (routed + shared.astype(jnp.float32)).astype(x.dtype)


def _layer(x, cache, w_in, conv_w, a_log, dt_bias, gdn_norm_w, w_out, ln1_g, ln1_b, w_router, router_bias,
           w_gate_e, w_up_e, w_down_e, w_gate_s, w_up_s, w_down_s, ln2_g, ln2_b):
    f32 = jnp.float32
    nb, t, _ = x.shape
    cuts = [int(c) for c in np.cumsum(PROJ_SIZES)[:-1]]
    qa, ka, va, qkv_b, z, b_raw, a_raw = jnp.split(x @ w_in, cuts, axis=-1)
    qa, ka, va = (u.reshape(nb, t, H_A, DH_A) for u in (qa, ka, va))
    if cache is None:
        o_a = _dilated_attention(qa, ka, va)
        keep = min(WIN_MAX, t)
        new_k, new_v = ka[:, t - keep:], va[:, t - keep:]
        conv_in = jnp.pad(qkv_b, ((0, 0), (CONV_W - 1, 0), (0, 0)))
        s0 = jnp.zeros((nb, H_B, DK, DV), f32)
    else:
        cache_k, cache_v, state_conv, state_ssm = cache
        o_a = _dilated_attention(qa, jnp.concatenate([cache_k, ka], axis=1),
                                 jnp.concatenate([cache_v, va], axis=1), n_past=cache_k.shape[1])
        new_k, new_v = ka, va
        conv_in = jnp.concatenate([state_conv, qkv_b], axis=1)
        s0 = state_ssm.astype(f32)
    new_conv = conv_in[:, -(CONV_W - 1):]
    qkv_c = jax.nn.silu(_causal_conv(conv_in, conv_w))
    q_b, k_b, v_b = jnp.split(qkv_c, [H_B * DK, 2 * H_B * DK], axis=-1)
    q_b = _l2norm(q_b.reshape(nb, t, H_B, DK)) * (DK ** -0.5)
    k_b = _l2norm(k_b.reshape(nb, t, H_B, DK))
    v_b = v_b.reshape(nb, t, H_B, DV).astype(f32)
    beta = jax.nn.sigmoid(b_raw.astype(f32))
    g = -jnp.exp(a_log.astype(f32)) * jax.nn.softplus(a_raw.astype(f32) + dt_bias.astype(f32))
    delta = _gated_delta_chunked if cache is None else _gated_delta_recurrent
    o_b, s_new = delta(q_b, k_b, v_b, g, beta, s0)
    o_b = o_b * lax.rsqrt(jnp.mean(jnp.square(o_b), -1, keepdims=True) + RMS_EPS) * gdn_norm_w.astype(f32)
    o_b = o_b * jax.nn.silu(z.reshape(nb, t, H_B, DV).astype(f32))
    mixed = jnp.concatenate([o_a.reshape(nb, t, ATT_WIDTH), o_b.reshape(nb, t, GDN_WIDTH)], axis=-1).astype(x.dtype)
    h = _layer_norm(ALPHA * x + mixed @ w_out, ln1_g, ln1_b)
    f = _moe(h.reshape(nb * t, D_MODEL), w_router, router_bias, w_gate_e, w_up_e, w_down_e,
             w_gate_s, w_up_s, w_down_s).reshape(h.shape)
    y = _layer_norm(ALPHA * h + f, ln2_g, ln2_b)
    return y, (new_k, new_v, new_conv.astype(x.dtype), s_new.astype(x.dtype))


def setup_inputs(seed: int = 0) -> dict:
    key = jax.random.key(seed)
    ks = jax.random.split(key, 32)
    f32 = jnp.float32

    def nrm(k, shape, scale):
        return jax.random.normal(k, shape, f32) * scale

    lb = min(WIN_MAX, PAST_LEN)
    col_scale = jnp.concatenate([jnp.ones(2 * ATT_WIDTH), jnp.full(ATT_WIDTH, BETA_INIT), jnp.ones(2 * H_B * DK),
                                 jnp.full(H_B * DV, BETA_INIT), jnp.ones(H_B * DV + 2 * H_B)]).astype(f32)
    dt = jnp.exp(jax.random.uniform(ks[9], (DEPTH, H_B), f32, math.log(1e-3), math.log(1e-1)))
    return {
        'x_prompt': nrm(ks[0], (BATCH, SEQ, D_MODEL), 1.0),
        'x_sample': nrm(ks[1], (DEC_BATCH, DEC_SEQ, D_MODEL), 1.0),
        'cache_k': nrm(ks[2], (DEPTH, DEC_BATCH, lb, H_A, DH_A), 1.0),
        'cache_v': nrm(ks[3], (DEPTH, DEC_BATCH, lb, H_A, DH_A), 1.0),
        'state_conv': nrm(ks[4], (DEPTH, DEC_BATCH, CONV_W - 1, C_CONV), 1.0),
        'state_ssm': nrm(ks[5], (DEPTH, DEC_BATCH, H_B, DK, DV), DK ** -0.5),
        'w_in': nrm(ks[6], (DEPTH, D_MODEL, PROJ_WIDTH), D_MODEL ** -0.5) * col_scale,
        'conv_w': nrm(ks[7], (DEPTH, CONV_W, C_CONV), CONV_W ** -0.5),
        'a_log': jnp.log(jax.random.uniform(ks[8], (DEPTH, H_B), f32, 1.0, 16.0)),
        'dt_bias': dt + jnp.log(-jnp.expm1(-dt)),
        'gdn_norm_w': 1.0 + nrm(ks[10], (DEPTH, DV), 0.02),
        'w_out': nrm(ks[11], (DEPTH, MIX_WIDTH, D_MODEL), MIX_WIDTH ** -0.5 * BETA_INIT),
        'ln1_g': 1.0 + nrm(ks[12], (DEPTH, D_MODEL), 0.02),
        'ln1_b': nrm(ks[13], (DEPTH, D_MODEL), 0.02),
        'w_router': nrm(ks[14], (DEPTH, D_MODEL, N_EXPERTS), D_MODEL ** -0.5),
        'router_bias': nrm(ks[15], (DEPTH, N_EXPERTS), 0.01),
        'w_gate_e': nrm(ks[16], (DEPTH, N_EXPERTS, D_MODEL, D_EXPERT), D_MODEL ** -0.5),
        'w_up_e': nrm(ks[17], (DEPTH, N_EXPERTS, D_MODEL, D_EXPERT), D_MODEL ** -0.5 * BETA_INIT),
        'w_down_e': nrm(ks[18], (DEPTH, N_EXPERTS, D_EXPERT, D_MODEL), D_EXPERT ** -0.5 * BETA_INIT),
        'w_gate_s': nrm(ks[19], (DEPTH, D_MODEL, D_SHARED), D_MODEL ** -0.5),
        'w_up_s': nrm(ks[20], (DEPTH, D_MODEL, D_SHARED), D_MODEL ** -0.5 * BETA_INIT),
        'w_down_s': nrm(ks[21], (DEPTH, D_SHARED, D_MODEL), D_SHARED ** -0.5 * BETA_INIT),
        'ln2_g': 1.0 + nrm(ks[22], (DEPTH, D_MODEL), 0.02),
        'ln2_b': nrm(ks[23], (DEPTH, D_MODEL), 0.02),
    }


def reference(x_prompt, x_sample, cache_k, cache_v, state_conv, state_ssm, w_in, conv_w, a_log, dt_bias,
              gdn_norm_w, w_out, ln1_g, ln1_b, w_router, router_bias, w_gate_e, w_up_e, w_down_e,
              w_gate_s, w_up_s, w_down_s, ln2_g, ln2_b):
    weights = (w_in, conv_w, a_log, dt_bias, gdn_norm_w, w_out, ln1_g, ln1_b, w_router, router_bias,
               w_gate_e, w_up_e, w_down_e, w_gate_s, w_up_s, w_down_s, ln2_g, ln2_b)
    y_prompt, y_sample = x_prompt, x_sample
    per_layer = []
    for layer in range(DEPTH):
        wl = [w[layer] for w in weights]
        y_prompt, st_p = _layer(y_prompt, None, *wl)
        y_sample, st_s = _layer(y_sample, (cache_k[layer], cache_v[layer], state_conv[layer], state_ssm[layer]), *wl)
        per_layer.append(st_p + st_s)
    k_p, v_p, conv_p, ssm_p, k_s, v_s, conv_s, ssm_s = [jnp.stack(col) for col in zip(*per_layer)]
    return (y_prompt, y_sample, k_p, v_p, conv_p, ssm_p, k_s, v_s, conv_s, ssm_s)
```

```python
import functools
import math

import jax
import jax.numpy as jnp
import numpy as np
from jax import lax
from jax.experimental import pallas as pl
from jax.experimental.pallas import tpu as pltpu

F32 = jnp.float32
BF16 = jnp.bfloat16
I32 = jnp.int32
U32 = jnp.uint32

ATT_PATTERNS = ((128, 1), (512, 4), (2048, 16))
DH_A = 64
DK = 128
DV = 128
CONV_W = 4
N_GROUP = 8
TOPK_GROUP = 4
TOP_K = 8
ROUTED_SCALE = 2.5
LN_EPS = 1e-5
RMS_EPS = 1e-6

LANES = 128
ROW_TILE = 128
ATT_BLOCK = 128
GDN_CHUNK = 64
MOE_BLOCK = 128
VMEM_LIMIT = 48 * 1024 * 1024

NEG_INF = float("-inf")


def _cparams(sem):
    return pltpu.CompilerParams(dimension_semantics=sem, vmem_limit_bytes=VMEM_LIMIT)


def _sigmoid(x):
    return 1.0 / (1.0 + jnp.exp(-x))


def _silu(x):
    return x * _sigmoid(x)


def _softplus(x):
    return jnp.maximum(x, 0.0) + jnp.log(1.0 + jnp.exp(-jnp.abs(x)))


def _split2(x):
    hi = x.astype(BF16)
    lo = (x - hi.astype(F32)).astype(BF16)
    return hi, lo


def _split3(x):
    hi = x.astype(BF16)
    r = x - hi.astype(F32)
    mid = r.astype(BF16)
    lo = (r - mid.astype(F32)).astype(BF16)
    return hi, mid, lo


def _dot(a, b):
    return jnp.dot(a, b, preferred_element_type=F32)


def _dot_nt(a, b):
    return lax.dot_general(a, b, (((1,), (1,)), ((), ())), preferred_element_type=F32)


def _dot_tn(a, b):
    return lax.dot_general(a, b, (((0,), (0,)), ((), ())), preferred_element_type=F32)


def _layer_norm_rows(x, g, b):
    mu = jnp.mean(x, axis=-1, keepdims=True)
    xc = x - mu
    var = jnp.mean(xc * xc, axis=-1, keepdims=True)
    return xc * lax.rsqrt(var + LN_EPS) * g + b


def _inproj_kernel(x_ref, wa_ref, wb_ref, wz_ref, wba_ref, qkva_ref, qkvb_ref, z_ref, ba_ref):
    xb = x_ref[...].astype(BF16)
    qkva_ref[...] = _dot(xb, wa_ref[...])
    qkvb_ref[...] = _dot(xb, wb_ref[...])
    z_ref[...] = _dot(xb, wz_ref[...])
    ba_ref[...] = _dot(xb, wba_ref[...])


def _inproj(x_all, w_in, att_w, conv_c, z_w):
    n, d = x_all.shape
    tm = 2 * ROW_TILE
    wa = w_in[:, :3 * att_w].astype(BF16)
    wb = w_in[:, 3 * att_w:3 * att_w + conv_c].astype(BF16)
    wz = w_in[:, 3 * att_w + conv_c:3 * att_w + conv_c + z_w].astype(BF16)
    w_ba = w_in[:, 3 * att_w + conv_c + z_w:]
    wba = jnp.pad(w_ba, ((0, 0), (0, LANES - w_ba.shape[1]))).astype(BF16)
    const = lambda i: (0, 0)
    row = lambda i: (i, 0)
    return pl.pallas_call(
        _inproj_kernel,
        grid=(pl.cdiv(n, tm),),
        in_specs=[
            pl.BlockSpec((tm, d), row),
            pl.BlockSpec(wa.shape, const),
            pl.BlockSpec(wb.shape, const),
            pl.BlockSpec(wz.shape, const),
            pl.BlockSpec(wba.shape, const),
        ],
        out_specs=[
            pl.BlockSpec((tm, wa.shape[1]), row),
            pl.BlockSpec((tm, wb.shape[1]), row),
            pl.BlockSpec((tm, wz.shape[1]), row),
            pl.BlockSpec((tm, LANES), row),
        ],
        out_shape=[
            jax.ShapeDtypeStruct((n, wa.shape[1]), F32),
            jax.ShapeDtypeStruct((n, wb.shape[1]), F32),
            jax.ShapeDtypeStruct((n, wz.shape[1]), F32),
            jax.ShapeDtypeStruct((n, LANES), F32),
        ],
        compiler_params=_cparams(("parallel",)),
        name="inproj",
    )(x_all, wa, wb, wz, wba)


def _alibi_slopes(n_heads):
    return 2.0 ** (-8.0 * np.arange(1, n_heads + 1, dtype=np.float64) / n_heads)


def _attn_bias(n_heads, dil, span):
    blk = ATT_BLOCK
    qi = np.arange(blk)[:, None]
    ki = np.arange(2 * blk)[None, :]
    j = blk + qi - ki
    band = (j >= 0) & (j <= span)
    slopes = _alibi_slopes(n_heads)
    per_head = np.where(band[None], -slopes[:, None, None] * (j * dil)[None].astype(np.float64), -np.inf)
    return jnp.asarray(per_head.reshape(n_heads // 2, 2 * blk, 2 * blk), F32)


def _attn_prompt_kernel(q_ref, kp_ref, kc_ref, vp_ref, vc_ref, bias_ref, o_ref, *, n_pairs, scale):
    blk = ATT_BLOCK
    i = pl.program_id(1)
    q = q_ref[...] * scale
    k = jnp.concatenate([kp_ref[...], kc_ref[...]], axis=0)
    v = jnp.concatenate([vp_ref[...], vc_ref[...]], axis=0)
    col = lax.broadcasted_iota(I32, (2 * blk, 2 * blk), 1)
    no_prev = col < jnp.where(i == 0, blk, 0)
    lane = lax.broadcasted_iota(I32, (blk, LANES), 1)
    low = lane < DH_A
    lse_tile = jnp.zeros((blk, LANES), F32)
    for p in range(n_pairs):
        sl = slice(p * LANES, (p + 1) * LANES)
        qp = q[:, sl]
        q2 = jnp.concatenate([jnp.where(low, qp, 0.0), jnp.where(low, 0.0, qp)], axis=0).astype(BF16)
        s = _dot_nt(q2, k[:, sl].astype(BF16)) + bias_ref[p]
        s = jnp.where(no_prev, NEG_INF, s)
        m = jnp.max(s, axis=-1, keepdims=True)
        e = jnp.exp(s - m)
        den = jnp.sum(e, axis=-1, keepdims=True)
        pv = _dot(e.astype(BF16), v[:, sl].astype(BF16)) * (1.0 / den)
        o_ref[:, sl] = jnp.where(low, pv[:blk], pv[blk:])
        lse = m + jnp.log(den)
        lse_tile = jnp.where(lane == 2 * p, lse[:blk], lse_tile)
        lse_tile = jnp.where(lane == 2 * p + 1, lse[blk:], lse_tile)
    o_ref[:, n_pairs * LANES:] = lse_tile


def _attn_prompt_branch(qkva, n_batch, seq, att_w, window, dil):
    blk = ATT_BLOCK
    span = window // dil
    assert span <= blk and seq % (dil * blk) == 0 and att_w % LANES == 0
    n_heads = att_w // DH_A
    n_pairs = att_w // LANES
    rows = qkva.shape[0]
    assert rows % dil == 0
    nblk = seq // (dil * blk)
    qv = qkva.reshape(rows // dil, dil * 3 * att_w)
    out_w = att_w + LANES
    bias = _attn_bias(n_heads, dil, span)

    def qmap(s, i):
        return (s // dil) * nblk + i, (s % dil) * 3

    def kpmap(s, i):
        return (s // dil) * nblk + jnp.maximum(i - 1, 0), (s % dil) * 3 + 1

    def kcmap(s, i):
        return (s // dil) * nblk + i, (s % dil) * 3 + 1

    def vpmap(s, i):
        return (s // dil) * nblk + jnp.maximum(i - 1, 0), (s % dil) * 3 + 2

    def vcmap(s, i):
        return (s // dil) * nblk + i, (s % dil) * 3 + 2

    def omap(s, i):
        return (s // dil) * nblk + i, s % dil

    out = pl.pallas_call(
        functools.partial(_attn_prompt_kernel, n_pairs=n_pairs, scale=DH_A ** -0.5),
        grid=(n_batch * dil, nblk),
        in_specs=[
            pl.BlockSpec((blk, att_w), qmap),
            pl.BlockSpec((blk, att_w), kpmap),
            pl.BlockSpec((blk, att_w), kcmap),
            pl.BlockSpec((blk, att_w), vpmap),
            pl.BlockSpec((blk, att_w), vcmap),
            pl.BlockSpec(bias.shape, lambda s, i: (0, 0, 0)),
        ],
        out_specs=pl.BlockSpec((blk, out_w), omap),
        out_shape=jax.ShapeDtypeStruct((n_batch * seq // dil, dil * out_w), F32),
        compiler_params=_cparams(("parallel", "arbitrary")),
        name=f"attn_prompt_d{dil}",
    )(qv, qv, qv, qv, qv, bias)
    return out.reshape(n_batch * seq, out_w)


def _unit_lower_inverse(a):
    n = a.shape[0]
    row = lax.broadcasted_iota(I32, (n, n), 0)
    col = lax.broadcasted_iota(I32, (n, n), 1)
    eye = jnp.where(row == col, 1.0, 0.0)
    d = eye - jnp.where((row // 2) == (col // 2), a, 0.0)
    s = 2
    while s < n:
        inside = jnp.logical_and((row // (2 * s)) == (col // (2 * s)), (row // s) != (col // s))
        l = jnp.where(inside, a, 0.0).astype(BF16)
        db = d.astype(BF16)
        d = d - _dot(db, _dot(l, db).astype(BF16))
        s *= 2
    return d


def _gdn_head_chunk(q, k, v, beta, gc, gc_rows, state):
    c = q.shape[0]
    q = q * lax.rsqrt(jnp.sum(q * q, axis=-1, keepdims=True) + RMS_EPS) * (DK ** -0.5)
    k = k * lax.rsqrt(jnp.sum(k * k, axis=-1, keepdims=True) + RMS_EPS)
    row = lax.broadcasted_iota(I32, (c, c), 0)
    col = lax.broadcasted_iota(I32, (c, c), 1)
    decay = jnp.exp(jnp.where(row >= col, gc - gc_rows, NEG_INF))
    kb = k * beta
    kbf = k.astype(BF16)
    a = jnp.where(row > col, _dot_nt(kb.astype(BF16), kbf) * decay, 0.0)
    eg = jnp.exp(gc)
    rhs = jnp.concatenate([v * beta, kb * eg], axis=1)
    t = _unit_lower_inverse(a)
    sol = _dot(t.astype(BF16), rhs.astype(BF16))
    u, w = sol[:, :DV], sol[:, DV:]
    qk = _dot_nt(q.astype(BF16), kbf) * decay
    g_last = gc[c - 1:c, :]
    k_dec = k * jnp.exp(g_last - gc)
    sb = state.astype(BF16)
    v_new = u - _dot(w.astype(BF16), sb)
    vb = v_new.astype(BF16)
    o = _dot((q * eg).astype(BF16), sb) + _dot(qk.astype(BF16), vb)
    new_state = state * jnp.exp(g_last) + _dot_tn(k_dec.astype(BF16), vb)
    return o, new_state


def _gated_rmsnorm(o, z, w):
    return o * lax.rsqrt(jnp.mean(o * o, axis=-1, keepdims=True) + RMS_EPS) * w * _silu(z)


def _gdn_prompt_kernel(*refs, n_batch, n_heads):
    qkv_refs = refs[:n_batch]
    z_refs = refs[n_batch:2 * n_batch]
    ba_refs = refs[2 * n_batch:3 * n_batch]
    convw_ref, alog_ref, dtb_ref, normw_ref, o_ref, s_ref, xext_ref, state_ref = refs[3 * n_batch:]
    c = GDN_CHUNK
    hk = n_heads * DK
    n = pl.program_id(0)

    @pl.when(n == 0)
    def _():
        xext_ref[:, 0:8, :] = jnp.zeros((n_batch, 8, xext_ref.shape[2]), F32)
        state_ref[...] = jnp.zeros(state_ref.shape, F32)

    row = lax.broadcasted_iota(I32, (c, c), 0)
    col = lax.broadcasted_iota(I32, (c, c), 1)
    tril = jnp.where(row >= col, 1.0, 0.0).astype(BF16)
    ones = jnp.ones((c, c), BF16)
    for b in range(n_batch):
        xext_ref[b, 8:8 + c, :] = qkv_refs[b][...]
        conv = xext_ref[b, pl.ds(8 - (CONV_W - 1), c), :] * convw_ref[0:1, :]
        for i in range(1, CONV_W):
            conv = conv + xext_ref[b, pl.ds(8 - (CONV_W - 1) + i, c), :] * convw_ref[i:i + 1, :]
        xext_ref[b, 0:8, :] = xext_ref[b, c:c + 8, :]
        act = _silu(conv)
        ba = ba_refs[b][...]
        beta_t = _sigmoid(ba)
        g_t = -jnp.exp(alog_ref[...]) * _softplus(ba + dtb_ref[...])
        g_hi, g_lo = _split2(g_t)
        gc_t = _dot(tril, g_hi) + _dot(tril, g_lo)
        z = z_refs[b][...]
        for h in range(n_heads):
            gc = gc_t[:, n_heads + h:n_heads + h + 1]
            diag = jnp.where(row == col, gc, 0.0)
            d_hi, d_mid, d_lo = _split3(diag)
            gc_rows = _dot(ones, d_hi) + _dot(ones, d_mid) + _dot(ones, d_lo)
            sl = slice(h * DK, (h + 1) * DK)
            o, new_state = _gdn_head_chunk(
                act[:, sl], act[:, hk + h * DK:hk + (h + 1) * DK], act[:, 2 * hk + h * DV:2 * hk + (h + 1) * DV],
                beta_t[:, h:h + 1], gc, gc_rows, state_ref[b * n_heads + h])
            state_ref[b * n_heads + h] = new_state
            o_ref[b, :, h * DV:(h + 1) * DV] = _gated_rmsnorm(o, z[:, h * DV:(h + 1) * DV], normw_ref[...])
    s_ref[...] = state_ref[...]


def _gdn_prompt(qkvb, z, ba, conv_w, a_log, dt_bias, norm_w, n_batch, seq):
    c = GDN_CHUNK
    n_heads = z.shape[1] // DV
    cc = qkvb.shape[1]
    nchunk = seq // c
    assert seq % c == 0
    alog_v = jnp.zeros((1, LANES), F32).at[0, n_heads:2 * n_heads].set(a_log.astype(F32))
    dtb_v = jnp.zeros((1, LANES), F32).at[0, n_heads:2 * n_heads].set(dt_bias.astype(F32))

    def rmap(b):
        return lambda n: (b * nchunk + n, 0)

    const = lambda n: (0, 0)
    in_specs = ([pl.BlockSpec((c, cc), rmap(b)) for b in range(n_batch)]
                + [pl.BlockSpec((c, z.shape[1]), rmap(b)) for b in range(n_batch)]
                + [pl.BlockSpec((c, LANES), rmap(b)) for b in range(n_batch)]
                + [pl.BlockSpec(conv_w.shape, const), pl.BlockSpec((1, LANES), const),
                   pl.BlockSpec((1, LANES), const), pl.BlockSpec((1, DV), const)])
    return pl.pallas_call(
        functools.partial(_gdn_prompt_kernel, n_batch=n_batch, n_heads=n_heads),
        grid=(nchunk,),
        in_specs=in_specs,
        out_specs=[
            pl.BlockSpec((n_batch, c, n_heads * DV), lambda n: (0, n, 0)),
            pl.BlockSpec((n_batch * n_heads, DK, DV), lambda n: (0, 0, 0)),
        ],
        out_shape=[
            jax.ShapeDtypeStruct((n_batch, seq, n_heads * DV), F32),
            jax.ShapeDtypeStruct((n_batch * n_heads, DK, DV), F32),
        ],
        scratch_shapes=[
            pltpu.VMEM((n_batch, c + 8, cc), F32),
            pltpu.VMEM((n_batch * n_heads, DK, DV), F32),
        ],
        compiler_params=_cparams(("arbitrary",)),
        name="gdn_prompt",
    )(*([qkvb] * n_batch), *([z] * n_batch), *([ba] * n_batch), conv_w, alog_v, dtb_v, norm_w.reshape(1, DV))


def _head_segments(att_w):
    seg = (np.arange(att_w)[:, None] // DH_A == np.arange(LANES)[None, :]).astype(np.float32)
    return jnp.asarray(seg, BF16), jnp.asarray(seg.T, BF16)


def _mm01(x, m01):
    hi, lo = _split2(x)
    return _dot(hi, m01) + _dot(lo, m01)


def _stack_rows(rows):
    rid = lax.broadcasted_iota(I32, (8, rows[0].shape[1]), 0)
    out = jnp.zeros((8, rows[0].shape[1]), F32)
    for r, x in enumerate(rows):
        out = jnp.where(rid == r, x, out)
    return out


def _attn_sample_kernel(*refs, n_branch, att_w, scale):
    qkv_ref = refs[0]
    kv_refs = refs[1:1 + 2 * n_branch]
    bias_ref, seg_ref, segt_ref, o_ref = refs[1 + 2 * n_branch:]
    b = pl.program_id(0)

    @pl.when(b == 0)
    def _():
        o_ref[...] = jnp.zeros(o_ref.shape, F32)

    row = qkv_ref[pl.ds(b, 1), :]
    q = row[:, :att_w] * scale
    k_new = row[:, att_w:2 * att_w]
    v_new = row[:, 2 * att_w:]
    seg = seg_ref[...]
    segt = segt_ref[...]
    s_new = _mm01(jnp.broadcast_to(q * k_new, (8, att_w)), seg)[0:1]
    outs, lses = [], []
    for br in range(n_branch):
        kb = kv_refs[2 * br][...]
        vb = kv_refs[2 * br + 1][...]
        s = _mm01(kb * q, seg) + bias_ref[br]
        m = jnp.maximum(jnp.max(s, axis=0, keepdims=True), s_new)
        p = jnp.exp(s - m)
        p_new = jnp.exp(s_new - m)
        den = jnp.sum(p, axis=0, keepdims=True) + p_new
        pe = _mm01(p, segt)
        small = _mm01(_stack_rows([p_new, den]), segt)
        num = jnp.sum(pe * vb, axis=0, keepdims=True) + small[0:1] * v_new
        outs.append(num / small[1:2])
        lses.append(m + jnp.log(den))
    m = functools.reduce(jnp.maximum, lses)
    ws = [jnp.exp(l - m) for l in lses]
    tot = functools.reduce(lambda a, c: a + c, ws)
    we = _mm01(_stack_rows([w / tot for w in ws]), segt)
    o = we[0:1] * outs[0]
    for br in range(1, n_branch):
        o = o + we[br:br + 1] * outs[br]
    o_ref[pl.ds(b, 1), :] = o


def _attn_sample(qkva, row_block, cache_k, cache_v, att_w):
    n_seq, n_past, _ = cache_k.shape
    n_heads = att_w // DH_A
    slopes = _alibi_slopes(n_heads)
    seg, segt = _head_segments(att_w)
    views, specs, biases = [], [], []
    for window, dil in ATT_PATTERNS:
        span = window // dil
        assert n_past >= window and n_past % dil == 0 and (n_past // dil) % span == 0
        last = n_past // dil // span - 1
        for c in (cache_k, cache_v):
            views.append(c.reshape(n_seq, n_past // dil, dil * att_w))
            specs.append(pl.BlockSpec((None, span, att_w), lambda b, last=last: (b, last, 0)))
        j = span - np.arange(span)
        bias = np.zeros((span, LANES), np.float64)
        bias[:, :n_heads] = -slopes[None, :] * (j * dil)[:, None]
        biases.append(bias)
    bias = jnp.asarray(np.stack(biases), F32)
    n_branch = len(ATT_PATTERNS)
    return pl.pallas_call(
        functools.partial(_attn_sample_kernel, n_branch=n_branch, att_w=att_w, scale=DH_A ** -0.5),
        grid=(n_seq,),
        in_specs=[pl.BlockSpec((ROW_TILE, 3 * att_w), lambda b: (row_block, 0))] + specs + [
            pl.BlockSpec(bias.shape, lambda b: (0, 0, 0)),
            pl.BlockSpec(seg.shape, lambda b: (0, 0)),
            pl.BlockSpec(segt.shape, lambda b: (0, 0)),
        ],
        out_specs=pl.BlockSpec((ROW_TILE, att_w), lambda b: (0, 0)),
        out_shape=jax.ShapeDtypeStruct((ROW_TILE, att_w), F32),
        compiler_params=_cparams(("arbitrary",)),
        name="attn_sample",
    )(qkva, *views, bias, seg, segt)


def _gdn_sample_kernel(qkv_ref, z_ref, ba_ref, sc_ref, ssm_ref, convw_ref, alog_ref, dtb_ref, normw_ref,
                       o_ref, s_ref, *, n_heads):
    b = pl.program_id(0)
    hk = n_heads * DK

    @pl.when(b == 0)
    def _():
        o_ref[...] = jnp.zeros(o_ref.shape, F32)

    row = qkv_ref[pl.ds(b, 1), :]
    conv = (jnp.sum(sc_ref[...] * convw_ref[0:CONV_W - 1, :], axis=0, keepdims=True)
            + row * convw_ref[CONV_W - 1:CONV_W, :])
    act = _silu(conv)
    ba = ba_ref[pl.ds(b, 1), :]
    beta_t = _sigmoid(ba)
    g_t = -jnp.exp(alog_ref[...]) * _softplus(ba + dtb_ref[...])
    z = z_ref[pl.ds(b, 1), :]
    outs = []
    for h in range(n_heads):
        q = act[:, h * DK:(h + 1) * DK]
        k = act[:, hk + h * DK:hk + (h + 1) * DK]
        v = act[:, 2 * hk + h * DV:2 * hk + (h + 1) * DV]
        q = q * lax.rsqrt(jnp.sum(q * q, axis=-1, keepdims=True) + RMS_EPS) * (DK ** -0.5)
        k = k * lax.rsqrt(jnp.sum(k * k, axis=-1, keepdims=True) + RMS_EPS)
        beta = beta_t[:, h:h + 1]
        g = g_t[:, n_heads + h:n_heads + h + 1]
        state = ssm_ref[h] * jnp.exp(g)
        k_col = jnp.broadcast_to(k, (DK, DK)).T
        q_col = jnp.broadcast_to(q, (DK, DK)).T
        v_t = (v - jnp.sum(k_col * state, axis=0, keepdims=True)) * beta
        state = state + k_col * v_t
        s_ref[h] = state
        o = jnp.sum(q_col * state, axis=0, keepdims=True)
        outs.append(_gated_rmsnorm(o, z[:, h * DV:(h + 1) * DV], normw_ref[...]))
    o_ref[pl.ds(b, 1), :] = jnp.concatenate(outs, axis=1)


def _gdn_sample(qkvb, z, ba, row_block, state_conv, state_ssm, conv_w, a_log, dt_bias, norm_w):
    n_seq, n_heads = state_ssm.shape[0], state_ssm.shape[1]
    cc = qkvb.shape[1]
    alog_v = jnp.zeros((1, LANES), F32).at[0, n_heads:2 * n_heads].set(a_log.astype(F32))
    dtb_v = jnp.zeros((1, LANES), F32).at[0, n_heads:2 * n_heads].set(dt_bias.astype(F32))
    rb = lambda b: (row_block, 0)
    const = lambda b: (0, 0)
    return pl.pallas_call(
        functools.partial(_gdn_sample_kernel, n_heads=n_heads),
        grid=(n_seq,),
        in_specs=[
            pl.BlockSpec((ROW_TILE, cc), rb),
            pl.BlockSpec((ROW_TILE, z.shape[1]), rb),
            pl.BlockSpec((ROW_TILE, LANES), rb),
            pl.BlockSpec((None, CONV_W - 1, cc), lambda b: (b, 0, 0)),
            pl.BlockSpec((None, n_heads, DK, DV), lambda b: (b, 0, 0, 0)),
            pl.BlockSpec(conv_w.shape, const),
            pl.BlockSpec((1, LANES), const),
            pl.BlockSpec((1, LANES), const),
            pl.BlockSpec((1, DV), const),
        ],
        out_specs=[
            pl.BlockSpec((ROW_TILE, n_heads * DV), const),
            pl.BlockSpec((None, n_heads, DK, DV), lambda b: (b, 0, 0, 0)),
        ],
        out_shape=[
            jax.ShapeDtypeStruct((ROW_TILE, n_heads * DV), F32),
            jax.ShapeDtypeStruct(state_ssm.shape, F32),
        ],
        compiler_params=_cparams(("arbitrary",)),
        name="gdn_sample",
    )(qkvb, z, ba, state_conv, state_ssm, conv_w, alog_v, dtb_v, norm_w.reshape(1, DV))


def _pack_bf16_pairs(h):
    half = h.shape[1] // 2
    bits = pltpu.bitcast(h.astype(BF16).astype(F32), U32)
    return (bits[:, :half] >> 16) | (bits[:, half:] & jnp.uint32(0xFFFF0000))


def _unpack_bf16_pairs(p):
    lo = pltpu.bitcast(p << 16, F32).astype(BF16)
    hi = pltpu.bitcast(p & jnp.uint32(0xFFFF0000), F32).astype(BF16)
    return lo, hi


def _outproj_kernel(*refs, n_branch, n_prompt_tiles, att_w, alpha):
    br_refs = refs[:n_branch]
    oas_ref, obp_ref, obs_ref, x_ref, w_ref, segt_ref, g_ref, b_ref, h_ref, hp_ref = refs[n_branch:]
    is_sample = pl.program_id(0) >= n_prompt_tiles
    segt = segt_ref[...]
    lses = [r[:, att_w:] for r in br_refs]
    m = functools.reduce(jnp.maximum, lses)
    ws = [jnp.exp(l - m) for l in lses]
    tot = functools.reduce(lambda a, c: a + c, ws)
    oa = None
    for r, w in zip(br_refs, ws):
        term = _mm01(w / tot, segt) * r[:, :att_w]
        oa = term if oa is None else oa + term
    oa = jnp.where(is_sample, oas_ref[...], oa)
    ob = jnp.where(is_sample, obs_ref[...], obp_ref[...])
    mixed = jnp.concatenate([oa, ob], axis=1).astype(BF16)
    y = _dot(mixed, w_ref[...]) + alpha * x_ref[...]
    h = _layer_norm_rows(y, g_ref[...], b_ref[...])
    h_ref[...] = h
    hp_ref[...] = _pack_bf16_pairs(h)


def _outproj_ln(branches, oa_s, ob_p, ob_s, x_all, w_out, g, b, alpha):
    n, d = x_all.shape
    n_prompt = ob_p.shape[0]
    tm = ROW_TILE
    assert n_prompt % tm == 0 and n == n_prompt + tm
    npt = n_prompt // tm
    att_w = oa_s.shape[1]
    _, segt = _head_segments(att_w)
    pmap = lambda i: (jnp.minimum(i, npt - 1), 0)
    const = lambda i: (0, 0)
    row = lambda i: (i, 0)
    wb = w_out.astype(BF16)
    return pl.pallas_call(
        functools.partial(_outproj_kernel, n_branch=len(branches), n_prompt_tiles=npt, att_w=att_w, alpha=alpha),
        grid=(npt + 1,),
        in_specs=[pl.BlockSpec((tm, br.shape[1]), pmap) for br in branches] + [
            pl.BlockSpec((tm, att_w), const),
            pl.BlockSpec((tm, ob_p.shape[1]), pmap),
            pl.BlockSpec((tm, ob_s.shape[1]), const),
            pl.BlockSpec((tm, d), row),
            pl.BlockSpec(wb.shape, const),
            pl.BlockSpec(segt.shape, const),
            pl.BlockSpec((1, d), const),
            pl.BlockSpec((1, d), const),
        ],
        out_specs=[pl.BlockSpec((tm, d), row), pl.BlockSpec((tm, d // 2), row)],
        out_shape=[jax.ShapeDtypeStruct((n, d), F32), jax.ShapeDtypeStruct((n, d // 2), U32)],
        compiler_params=_cparams(("parallel",)),
        name="outproj_ln1",
    )(*branches, oa_s, ob_p, ob_s, x_all, wb, segt, g.reshape(1, d), b.reshape(1, d))


def _router_kernel(h_ref, wrh_ref, wrl_ref, bias_ref, eidx_ref, gate_ref, rank_ref, cnt_ref, base_ref):
    i = pl.program_id(0)

    @pl.when(i == 0)
    def _():
        base_ref[...] = jnp.zeros(base_ref.shape, F32)

    tt = h_ref.shape[0]
    ne = wrh_ref.shape[0]
    gsz = ne // N_GROUP
    h_hi, h_lo = _split2(h_ref[...])
    wrh = wrh_ref[...]
    logits = _dot_nt(wrh, h_hi) + _dot_nt(wrh, h_lo) + _dot_nt(wrl_ref[...], h_hi)
    scores = _sigmoid(logits)
    choice = scores + bias_ref[...]
    gid = lax.broadcasted_iota(I32, (N_GROUP, tt), 0)
    gs = jnp.zeros((N_GROUP, tt), F32)
    for g in range(N_GROUP):
        cg = choice[g * gsz:(g + 1) * gsz, :]
        m1 = jnp.max(cg, axis=0, keepdims=True)
        n1 = jnp.sum(jnp.where(cg == m1, 1.0, 0.0), axis=0, keepdims=True)
        m2 = jnp.max(jnp.where(cg < m1, cg, NEG_INF), axis=0, keepdims=True)
        gs = jnp.where(gid == g, m1 + jnp.where(n1 >= 2.0, m1, m2), gs)
    ahead = jnp.zeros((N_GROUP, tt), F32)
    for g in range(N_GROUP):
        r = gs[g:g + 1, :]
        before = jnp.logical_or(r > gs, jnp.logical_and(r == gs, gid > g))
        ahead = ahead + jnp.where(before, 1.0, 0.0)
    keep = jnp.where(ahead < float(TOPK_GROUP), 1.0, 0.0)
    masked = jnp.concatenate(
        [jnp.where(keep[g:g + 1, :] > 0.5, choice[g * gsz:(g + 1) * gsz, :], NEG_INF) for g in range(N_GROUP)], axis=0)
    eid = lax.broadcasted_iota(I32, (ne, tt), 0).astype(F32)
    kid = lax.broadcasted_iota(I32, (TOP_K, tt), 0)
    onehot = jnp.zeros((ne, tt), F32)
    idxs = []
    e_tile = jnp.zeros((TOP_K, tt), F32)
    s_tile = jnp.zeros((TOP_K, tt), F32)
    for k in range(TOP_K):
        m = jnp.max(masked, axis=0, keepdims=True)
        idx = jnp.min(jnp.where(masked == m, eid, float(ne)), axis=0, keepdims=True)
        sel = eid == idx
        s_k = jnp.sum(jnp.where(sel, scores, 0.0), axis=0, keepdims=True)
        masked = jnp.where(sel, NEG_INF, masked)
        onehot = onehot + jnp.where(sel, 1.0, 0.0)
        idxs.append(idx)
        e_tile = jnp.where(kid == k, idx, e_tile)
        s_tile = jnp.where(kid == k, s_k, s_tile)
    tot = jnp.sum(s_tile, axis=0, keepdims=True)
    tr = lax.broadcasted_iota(I32, (tt, tt), 0)
    tc = lax.broadcasted_iota(I32, (tt, tt), 1)
    earlier = jnp.where(tr < tc, 1.0, 0.0).astype(BF16)
    before = _dot(onehot.astype(BF16), earlier) + base_ref[...]
    r_tile = jnp.zeros((TOP_K, tt), F32)
    for k in range(TOP_K):
        r_k = jnp.sum(jnp.where(eid == idxs[k], before, 0.0), axis=0, keepdims=True)
        r_tile = jnp.where(kid == k, r_k, r_tile)
    base_ref[...] = base_ref[...] + jnp.sum(onehot, axis=1, keepdims=True)
    eidx_ref[0] = e_tile.astype(I32)
    gate_ref[0] = s_tile / tot * ROUTED_SCALE
    rank_ref[0] = r_tile.astype(I32)
    cnt_ref[...] = base_ref[...]


def _router(h_all, w_router, router_bias):
    n, d = h_all.shape
    ne = w_router.shape[1]
    tt = ROW_TILE
    nt = n // tt
    wt = w_router.astype(F32).T
    wrh = wt.astype(BF16)
    wrl = (wt - wrh.astype(F32)).astype(BF16)
    const = lambda i: (0, 0)
    tile = lambda i: (i, 0, 0)
    return pl.pallas_call(
        _router_kernel,
        grid=(nt,),
        in_specs=[
            pl.BlockSpec((tt, d), lambda i: (i, 0)),
            pl.BlockSpec((ne, d), const),
            pl.BlockSpec((ne, d), const),
            pl.BlockSpec((ne, 1), const),
        ],
        out_specs=[
            pl.BlockSpec((1, TOP_K, tt), tile),
            pl.BlockSpec((1, TOP_K, tt), tile),
            pl.BlockSpec((1, TOP_K, tt), tile),
            pl.BlockSpec((ne, LANES), const),
        ],
        out_shape=[
            jax.ShapeDtypeStruct((nt, TOP_K, tt), I32),
            jax.ShapeDtypeStruct((nt, TOP_K, tt), F32),
            jax.ShapeDtypeStruct((nt, TOP_K, tt), I32),
            jax.ShapeDtypeStruct((ne, LANES), F32),
        ],
        scratch_shapes=[pltpu.VMEM((ne, LANES), F32)],
        compiler_params=_cparams(("arbitrary",)),
        name="moe_router",
    )(h_all, wrh, wrl, router_bias.astype(F32).reshape(ne, 1))


def _load_tile_indices(i, e_hbm, r_hbm, e_smem, r_smem, sem):
    ce = pltpu.make_async_copy(e_hbm.at[i], e_smem, sem.at[0])
    cr = pltpu.make_async_copy(r_hbm.at[i], r_smem, sem.at[1])
    ce.start()
    cr.start()
    ce.wait()
    cr.wait()


def _dispatch_kernel(off_ref, hp_ref, e_hbm, r_hbm, xs_in, xs_out, e_smem, r_smem, idx_sem, row_sem):
    del xs_in
    i = pl.program_id(0)
    tt = hp_ref.shape[0]
    _load_tile_indices(i, e_hbm, r_hbm, e_smem, r_smem, idx_sem)

    def body(t, carry):
        for k in range(TOP_K):
            p = off_ref[e_smem[k * tt + t]] + r_smem[k * tt + t]
            pltpu.make_async_copy(hp_ref.at[pl.ds(t, 1)], xs_out.at[pl.ds(p, 1)], row_sem).start()
        return carry

    lax.fori_loop(0, tt, body, 0)
    for k in range(TOP_K):
        pltpu.make_async_copy(hp_ref, xs_out.at[pl.ds(0, tt)], row_sem).wait()


def _dispatch(hp_all, e_tiles, r_tiles, pad_off, n_rows):
    n, w = hp_all.shape
    tt = ROW_TILE
    nt = n // tt
    zeros = jnp.zeros((n_rows, w), U32)
    grid_spec = pltpu.PrefetchScalarGridSpec(
        num_scalar_prefetch=1,
        grid=(nt,),
        in_specs=[
            pl.BlockSpec((tt, w), lambda i, off: (i, 0)),
            pl.BlockSpec(memory_space=pl.ANY),
            pl.BlockSpec(memory_space=pl.ANY),
            pl.BlockSpec(memory_space=pl.ANY),
        ],
        out_specs=pl.BlockSpec(memory_space=pl.ANY),
        scratch_shapes=[
            pltpu.SMEM((TOP_K * tt,), I32),
            pltpu.SMEM((TOP_K * tt,), I32),
            pltpu.SemaphoreType.DMA((2,)),
            pltpu.SemaphoreType.DMA,
        ],
    )
    return pl.pallas_call(
        _dispatch_kernel,
        grid_spec=grid_spec,
        out_shape=jax.ShapeDtypeStruct((n_rows, w), U32),
        input_output_aliases={4: 0},
        compiler_params=_cparams(("arbitrary",)),
        name="moe_dispatch",
    )(pad_off, hp_all, e_tiles, r_tiles, zeros)


def _experts_kernel(blk_e_ref, nblk_ref, xs_ref, wg_ref, wu_ref, wd_ref, y_ref):
    del blk_e_ref
    active = pl.program_id(0) < nblk_ref[0]

    @pl.when(jnp.logical_not(active))
    def _():
        y_ref[...] = jnp.zeros(y_ref.shape, F32)

    @pl.when(active)
    def _():
        half = xs_ref.shape[1]
        lo, hi = _unpack_bf16_pairs(xs_ref[...])
        wg = wg_ref[...].astype(BF16)
        wu = wu_ref[...].astype(BF16)
        gate = _dot(lo, wg[:half]) + _dot(hi, wg[half:])
        up = _dot(lo, wu[:half]) + _dot(hi, wu[half:])
        hdn = (_silu(gate) * up).astype(BF16)
        y_ref[...] = _dot(hdn, wd_ref[...].astype(BF16))


def _experts(xs, blk_e, nblk, w_gate_e, w_up_e, w_down_e):
    n_rows, half = xs.shape
    ne, d, de = w_gate_e.shape
    nb = n_rows // MOE_BLOCK

    def rows(b, blk_e, nblk):
        return jnp.minimum(b, nblk[0] - 1), 0

    def wmap(b, blk_e, nblk):
        return blk_e[b], 0, 0

    grid_spec = pltpu.PrefetchScalarGridSpec(
        num_scalar_prefetch=2,
        grid=(nb,),
        in_specs=[
            pl.BlockSpec((MOE_BLOCK, half), rows),
            pl.BlockSpec((None, d, de), wmap),
            pl.BlockSpec((None, d, de), wmap),
            pl.BlockSpec((None, de, d), wmap),
        ],
        out_specs=pl.BlockSpec((MOE_BLOCK, d), lambda b, blk_e, nblk: (b, 0)),
    )
    return pl.pallas_call(
        _experts_kernel,
        grid_spec=grid_spec,
        out_shape=jax.ShapeDtypeStruct((n_rows, d), F32),
        compiler_params=_cparams(("arbitrary",)),
        name="moe_experts",
    )(blk_e, nblk, xs, w_gate_e, w_up_e, w_down_e)


def _combine_kernel(off_ref, h_ref, gate_ref, wgs_ref, wus_ref, wds_ref, g_ref, b_ref, e_hbm, r_hbm, ys_hbm,
                    y_ref, ybuf, e_smem, r_smem, idx_sem, row_sem, *, alpha):
    i = pl.program_id(0)
    tt = h_ref.shape[0]
    _load_tile_indices(i, e_hbm, r_hbm, e_smem, r_smem, idx_sem)

    def body(t, carry):
        for k in range(TOP_K):
            p = off_ref[e_smem[k * tt + t]] + r_smem[k * tt + t]
            pltpu.make_async_copy(ys_hbm.at[pl.ds(p, 1)], ybuf.at[k, pl.ds(t, 1)], row_sem).start()
        return carry

    lax.fori_loop(0, tt, body, 0)
    h = h_ref[...]
    hb = h.astype(BF16)
    hdn = (_silu(_dot(hb, wgs_ref[...])) * _dot(hb, wus_ref[...])).astype(BF16)
    f = _dot(hdn, wds_ref[...])
    for k in range(TOP_K):
        pltpu.make_async_copy(ys_hbm.at[pl.ds(0, tt)], ybuf.at[k], row_sem).wait()
    gate = gate_ref[...]
    for k in range(TOP_K):
        f = f + gate[:, k:k + 1] * ybuf[k]
    y_ref[...] = _layer_norm_rows(alpha * h + f, g_ref[...], b_ref[...])


def _combine(h_all, gate_t, e_tiles, r_tiles, pad_off, ys, w_gate_s, w_up_s, w_down_s, g, b, alpha):
    n, d = h_all.shape
    tt = ROW_TILE
    nt = n // tt
    ds = w_gate_s.shape[1]
    row = lambda i, off: (i, 0)
    const = lambda i, off: (0, 0)
    grid_spec = pltpu.PrefetchScalarGridSpec(
        num_scalar_prefetch=1,
        grid=(nt,),
        in_specs=[
            pl.BlockSpec((tt, d), row),
            pl.BlockSpec((tt, TOP_K), row),
            pl.BlockSpec((d, ds), const),
            pl.BlockSpec((d, ds), const),
            pl.BlockSpec((ds, d), const),
            pl.BlockSpec((1, d), const),
            pl.BlockSpec((1, d), const),
            pl.BlockSpec(memory_space=pl.ANY),
            pl.BlockSpec(memory_space=pl.ANY),
            pl.BlockSpec(memory_space=pl.ANY),
        ],
        out_specs=pl.BlockSpec((tt, d), row),
        scratch_shapes=[
            pltpu.VMEM((TOP_K, tt, d), F32),
            pltpu.SMEM((TOP_K * tt,), I32),
            pltpu.SMEM((TOP_K * tt,), I32),
            pltpu.SemaphoreType.DMA((2,)),
            pltpu.SemaphoreType.DMA,
        ],
    )
    return pl.pallas_call(
        functools.partial(_combine_kernel, alpha=alpha),
        grid_spec=grid_spec,
        out_shape=jax.ShapeDtypeStruct((n, d), F32),
        compiler_params=_cparams(("arbitrary",)),
        name="moe_combine",
    )(pad_off, h_all, gate_t, w_gate_s.astype(BF16), w_up_s.astype(BF16), w_down_s.astype(BF16),
      g.reshape(1, d), b.reshape(1, d), e_tiles, r_tiles, ys)


def _moe(h_all, hp_all, w_router, router_bias, w_gate_e, w_up_e, w_down_e, w_gate_s, w_up_s, w_down_s, g, b, alpha):
    n, d = h_all.shape
    ne = w_router.shape[1]
    tt = ROW_TILE
    nt = n // tt
    e3, gate3, r3, cnt = _router(h_all, w_router, router_bias)
    counts = cnt[:, 0].astype(I32)
    padded = (counts + MOE_BLOCK - 1) // MOE_BLOCK * MOE_BLOCK
    pad_end = jnp.cumsum(padded)
    pad_off = (pad_end - padded).astype(I32)
    nb = n * TOP_K // MOE_BLOCK + ne - 1
    nblk = (pad_end[-1] // MOE_BLOCK).astype(I32)
    blk_ids = jnp.minimum(jnp.arange(nb, dtype=I32), nblk - 1)
    blk_e = jnp.minimum(jnp.searchsorted(pad_end, blk_ids * MOE_BLOCK, side="right"), ne - 1).astype(I32)
    e_tiles = e3.reshape(nt, TOP_K * tt)
    r_tiles = r3.reshape(nt, TOP_K * tt)
    gate_t = jnp.transpose(gate3, (0, 2, 1)).reshape(n, TOP_K)
    xs = _dispatch(hp_all, e_tiles, r_tiles, pad_off, nb * MOE_BLOCK)
    ys = _experts(xs, blk_e, nblk.reshape(1), w_gate_e, w_up_e, w_down_e)
    return _combine(h_all, gate_t, e_tiles, r_tiles, pad_off, ys, w_gate_s, w_up_s, w_down_s, g, b, alpha)


def kernel(x_prompt, x_sample, cache_k, cache_v, state_conv, state_ssm, w_in, conv_w, a_log, dt_bias, gdn_norm_w, w_out, ln1_g, ln1_b, w_router, router_bias, w_gate_e, w_up_e, w_down_e, w_gate_s, w_up_s, w_down_s, ln2_g, ln2_b):
    depth = w_in.shape[0]
    assert depth == 1 and x_sample.shape[1] == 1
    n_batch, seq, d = x_prompt.shape
    n_seq = x_sample.shape[0]
    n_past = cache_k.shape[2]
    n_heads_a = cache_k.shape[3]
    att_w = n_heads_a * cache_k.shape[4]
    n_heads_b = state_ssm.shape[2]
    conv_c = conv_w.shape[2]
    n_prompt = n_batch * seq
    assert n_prompt % ROW_TILE == 0 and n_seq <= ROW_TILE
    alpha = (2 * depth) ** 0.25
    dt = x_prompt.dtype

    x_all = jnp.concatenate([x_prompt.reshape(n_prompt, d), x_sample.reshape(n_seq, d),
                             jnp.zeros((ROW_TILE - n_seq, d), dt)], axis=0)
    qkva, qkvb, z, ba = _inproj(x_all, w_in[0], att_w, conv_c, n_heads_b * DV)
    sample_block = n_prompt // ROW_TILE

    branches = [_attn_prompt_branch(qkva, n_batch, seq, att_w, window, dil) for window, dil in ATT_PATTERNS]
    oa_s = _attn_sample(qkva, sample_block, cache_k[0].reshape(n_seq, n_past, att_w),
                        cache_v[0].reshape(n_seq, n_past, att_w), att_w)
    ob_p, ssm_p = _gdn_prompt(qkvb, z, ba, conv_w[0], a_log[0], dt_bias[0], gdn_norm_w[0], n_batch, seq)
    ob_s, ssm_s = _gdn_sample(qkvb, z, ba, sample_block, state_conv[0], state_ssm[0], conv_w[0], a_log[0],
                              dt_bias[0], gdn_norm_w[0])
    h_all, hp_all = _outproj_ln(branches, oa_s, ob_p.reshape(n_prompt, n_heads_b * DV), ob_s, x_all, w_out[0],
                                ln1_g[0], ln1_b[0], alpha)
    y_all = _moe(h_all, hp_all, w_router[0], router_bias[0], w_gate_e[0], w_up_e[0], w_down_e[0],
                 w_gate_s[0], w_up_s[0], w_down_s[0], ln2_g[0], ln2_b[0], alpha)

    keep = min(max(w for w, _ in ATT_PATTERNS), seq)
    ka_p = qkva[:n_prompt, att_w:2 * att_w].reshape(n_batch, seq, n_heads_a, DH_A)
    va_p = qkva[:n_prompt, 2 * att_w:].reshape(n_batch, seq, n_heads_a, DH_A)
    qkvb_s = qkvb[n_prompt:n_prompt + n_seq]
    return (
        y_all[:n_prompt].reshape(n_batch, seq, d),
        y_all[n_prompt:n_prompt + n_seq].reshape(n_seq, 1, d),
        ka_p[None, :, seq - keep:],
        va_p[None, :, seq - keep:],
        qkvb[:n_prompt].reshape(n_batch, seq, conv_c)[None, :, seq - (CONV_W - 1):],
        ssm_p.reshape(1, n_batch, n_heads_b, DK, DV),
        qkva[n_prompt:n_prompt + n_seq, att_w:2 * att_w].reshape(1, n_seq, 1, n_heads_a, DH_A),
        qkva[n_prompt:n_prompt + n_seq, 2 * att_w:].reshape(1, n_seq, 1, n_heads_a, DH_A),
        jnp.concatenate([state_conv[0][:, 1:], qkvb_s[:, None, :]], axis=1)[None],
        ssm_s[None],
    )
```
